```python
import math
import jax, jax.numpy as jnp
from jax import lax
import numpy as np

D_MODEL = 1024
BATCH = 2
SEQ = 8192
DEPTH = 4

GRID_W = 64
CTX_LEN = 256
Q_BLOCK = 128
EPS = 1e-6
ROPE_THETA = 10000.0

A_HEADS = 6
A_KV_HEADS = 2
A_GROUP = A_HEADS // A_KV_HEADS
A_HEAD_DIM = 128
A_Q_WIDTH = A_HEADS * A_HEAD_DIM
A_KV_WIDTH = A_KV_HEADS * A_HEAD_DIM
B_GROUPS = 4
B_GROUP_DIM = 64
B_WIDTH = B_GROUPS * B_GROUP_DIM
EVEN_IN = A_Q_WIDTH + 2 * A_KV_WIDTH + B_WIDTH
EVEN_OUT = A_Q_WIDTH + B_WIDTH
EVEN_SPLIT = (A_Q_WIDTH, A_Q_WIDTH + A_KV_WIDTH, A_Q_WIDTH + 2 * A_KV_WIDTH)
C_HEADS = 8
C_QK_DIM = 64
C_V_DIM = 128
C_QK_WIDTH = C_HEADS * 2 * C_QK_DIM
C_V_WIDTH = C_HEADS * C_V_DIM
ODD_IN = 2 * C_QK_WIDTH + C_V_WIDTH
ODD_OUT = C_V_WIDTH
N_GROUPS = 4
EXPERTS_PER_GROUP = 4
N_EXPERTS = N_GROUPS * EXPERTS_PER_GROUP
TOP_K = 2
D_EXPERT = 512

kernel_name = "hybrid_gqa_fnet_diffattn_hmoe_dit"


def rms_norm(x, gain=None):
    xf = x.astype(jnp.float32)
    y = xf * lax.rsqrt(jnp.mean(xf * xf, axis=-1, keepdims=True) + EPS)
    if gain is not None:
        y = y * gain.astype(jnp.float32)
    return y.astype(x.dtype)


def axial_rope_tables(n_tokens, dim):
    rows = n_tokens // GRID_W
    row = jnp.broadcast_to(jnp.arange(rows, dtype=jnp.float32)[:, None], (rows, GRID_W)).reshape(-1)
    col = jnp.broadcast_to(jnp.arange(GRID_W, dtype=jnp.float32)[None, :], (rows, GRID_W)).reshape(-1)
    half = dim // 2
    inv_freq = ROPE_THETA ** (-jnp.arange(0, half, 2, dtype=jnp.float32) / half)
    ang_r = row[:, None] * inv_freq
    ang_c = col[:, None] * inv_freq
    return (jnp.cos(ang_r), jnp.sin(ang_r), jnp.cos(ang_c), jnp.sin(ang_c))


def _rotate(x, cos, sin):
    x1, x2 = jnp.split(x, 2, axis=-1)
    c = cos[:, None, :].astype(x.dtype)
    s = sin[:, None, :].astype(x.dtype)
    return jnp.concatenate([x1 * c - x2 * s, x1 * s + x2 * c], axis=-1)


def apply_axial_rope(x, tables):
    cr, sr, cc, sc = tables
    xr, xc = jnp.split(x, 2, axis=-1)
    return jnp.concatenate([_rotate(xr, cr, sr), _rotate(xc, cc, sc)], axis=-1)


def gqa_blocked(q, k, v):
    b, n = q.shape[:2]
    nb = n // Q_BLOCK
    qb = q.reshape(b, nb, Q_BLOCK, A_KV_HEADS, A_GROUP, A_HEAD_DIM).swapaxes(0, 1)
    scale = A_HEAD_DIM ** -0.5

    def block(qi):
        s = jnp.einsum('bqhgd,bkhd->bhgqk', qi, k).astype(jnp.float32) * scale
        p = jax.nn.softmax(s, axis=-1).astype(v.dtype)
        return jnp.einsum('bhgqk,bkhd->bqhgd', p, v)

    o = lax.map(block, qb)
    return o.swapaxes(0, 1).reshape(b, n, A_Q_WIDTH)


def diff_blocked(q, k, v, lam):
    b, n = q.shape[:2]
    nb = n // Q_BLOCK
    qb = q.reshape(b, nb, Q_BLOCK, C_HEADS, 2, C_QK_DIM).swapaxes(0, 1)
    scale = C_QK_DIM ** -0.5

    def block(qi):
        s = jnp.einsum('bqhmd,bkhmd->bmhqk', qi, k).astype(jnp.float32) * scale
        p = jax.nn.softmax(s, axis=-1)
        diff = (p[:, 0] - lam * p[:, 1]).astype(v.dtype)
        return jnp.einsum('bhqk,bkhd->bqhd', diff, v)

    o = lax.map(block, qb)
    return o.swapaxes(0, 1).reshape(b, n, C_HEADS, C_V_DIM)


def fourier_mix(f, gain):
    b, n, _ = f.shape
    g = rms_norm(f.reshape(b, n, B_GROUPS, B_GROUP_DIM), gain)
    spec = jnp.fft.fft2(g.astype(jnp.float32), axes=(1, 3), norm='ortho')
    return jnp.real(spec).astype(f.dtype).reshape(b, n, B_WIDTH)


def even_mixer(h_lat, h_ctx, w_in, w_out, q_gain, k_gain, f_gain, rope, need_ctx_out):
    def project(h):
        p = h @ w_in
        b, n, _ = p.shape
        q, k, v, f = jnp.split(p, EVEN_SPLIT, axis=-1)
        q = rms_norm(q.reshape(b, n, A_HEADS, A_HEAD_DIM), q_gain)
        k = rms_norm(k.reshape(b, n, A_KV_HEADS, A_HEAD_DIM), k_gain)
        v = v.reshape(b, n, A_KV_HEADS, A_HEAD_DIM)
        return q, k, v, f

    q_l, k_l, v_l, f_l = project(h_lat)
    q_c, k_c, v_c, f_c = project(h_ctx)
    q_l = apply_axial_rope(q_l, rope)
    k_l = apply_axial_rope(k_l, rope)
    k_all = jnp.concatenate([k_c, k_l], axis=1)
    v_all = jnp.concatenate([v_c, v_l], axis=1)
    y_lat = jnp.concatenate([gqa_blocked(q_l, k_all, v_all), fourier_mix(f_l, f_gain)], axis=-1) @ w_out
    if not need_ctx_out:
        return y_lat, None
    y_ctx = jnp.concatenate([gqa_blocked(q_c, k_c, v_c), fourier_mix(f_c, f_gain)], axis=-1) @ w_out
    return y_lat, y_ctx


def odd_mixer(h_lat, h_ctx, w_in, w_out, lq1, lk1, lq2, lk2, head_gain, lam_init, rope, need_ctx_out):
    f32 = jnp.float32
    lam = (jnp.exp(jnp.sum(lq1.astype(f32) * lk1.astype(f32)))
           - jnp.exp(jnp.sum(lq2.astype(f32) * lk2.astype(f32))) + lam_init)

    def project(h):
        p = h @ w_in
        b, n, _ = p.shape
        q, k, v = jnp.split(p, (C_QK_WIDTH, 2 * C_QK_WIDTH), axis=-1)
        return (q.reshape(b, n, 2 * C_HEADS, C_QK_DIM), k.reshape(b, n, 2 * C_HEADS, C_QK_DIM),
                v.reshape(b, n, C_HEADS, C_V_DIM))

    def to_maps(t):
        b, n = t.shape[:2]
        return t.reshape(b, n, C_HEADS, 2, C_QK_DIM)

    def finish(o):
        b, n = o.shape[:2]
        o = rms_norm(o, head_gain) * (1.0 - lam_init)
        return o.reshape(b, n, C_V_WIDTH) @ w_out

    q_l, k_l, v_l = project(h_lat)
    q_c, k_c, v_c = project(h_ctx)
    q_l = to_maps(apply_axial_rope(q_l, rope))
    k_l = to_maps(apply_axial_rope(k_l, rope))
    q_c, k_c = to_maps(q_c), to_maps(k_c)
    k_all = jnp.concatenate([k_c, k_l], axis=1)
    v_all = jnp.concatenate([v_c, v_l], axis=1)
    y_lat = finish(diff_blocked(q_l, k_all, v_all, lam))
    if not need_ctx_out:
        return y_lat, None
    return y_lat, finish(diff_blocked(q_c, k_c, v_c, lam))


def hierarchical_moe(h, w_group, b_group, w_router, b_router, w_gate, w_up, w_down):
    shape = h.shape
    tok = h.reshape(-1, shape[-1])
    g_logits = (tok @ w_group + b_group).astype(jnp.float32)
    g_prob = jax.nn.softmax(g_logits, axis=-1)
    g_idx = jnp.argmax(g_logits, axis=-1)
    g_gate = jnp.take_along_axis(g_prob, g_idx[:, None], axis=-1)
    e_logits = (tok @ w_router + b_router).astype(jnp.float32).reshape(-1, N_GROUPS, EXPERTS_PER_GROUP)
    e_sel = jnp.take_along_axis(e_logits, g_idx[:, None, None], axis=1)[:, 0]
    top_p, top_i = lax.top_k(jax.nn.softmax(e_sel, axis=-1), TOP_K)
    weights = g_gate * top_p / jnp.sum(top_p, axis=-1, keepdims=True)
    expert_id = g_idx[:, None] * EXPERTS_PER_GROUP + top_i
    dense_gate = jnp.einsum('nk,nke->en', weights, jax.nn.one_hot(expert_id, N_EXPERTS, dtype=jnp.float32))

    def expert_step(acc, params):
        wg, wu, wd, gate = params
        y = (jax.nn.silu(tok @ wg) * (tok @ wu)) @ wd
        return acc + gate[:, None].astype(y.dtype) * y, None

    out, _ = lax.scan(expert_step, jnp.zeros_like(tok), (w_gate, w_up, w_down, dense_gate))
    return out.reshape(shape)


def setup_inputs(seed: int = 0) -> dict:
    key = jax.random.key(seed)
    ks = jax.random.split(key, 26)
    n_even = (DEPTH + 1) // 2
    n_odd = DEPTH // 2

    def nrm(k, shape, s):
        return jax.random.normal(k, shape, jnp.float32) * s

    def gain(k, shape):
        return 1.0 + 0.02 * jax.random.normal(k, shape, jnp.float32)

    return {
        "x": nrm(ks[0], (BATCH, SEQ, D_MODEL), 1.0),
        "c": nrm(ks[1], (BATCH, D_MODEL), 1.0),
        "ctx": nrm(ks[2], (BATCH, CTX_LEN, D_MODEL), 1.0),
        "c_ctx": nrm(ks[3], (D_MODEL,), 1.0),
        "w_ada": nrm(ks[4], (DEPTH, D_MODEL, 6 * D_MODEL), 0.5 * D_MODEL ** -0.5),
        "b_ada": nrm(ks[5], (DEPTH, 6 * D_MODEL), 0.01),
        "w_in_even": nrm(ks[6], (n_even, D_MODEL, EVEN_IN), D_MODEL ** -0.5),
        "w_out_even": nrm(ks[7], (n_even, EVEN_OUT, D_MODEL), EVEN_OUT ** -0.5),
        "a_q_gain": gain(ks[8], (n_even, A_HEAD_DIM)),
        "a_k_gain": gain(ks[9], (n_even, A_HEAD_DIM)),
        "b_gain": gain(ks[10], (n_even, B_GROUPS, B_GROUP_DIM)),
        "w_in_odd": nrm(ks[11], (n_odd, D_MODEL, ODD_IN), D_MODEL ** -0.5),
        "w_out_odd": nrm(ks[12], (n_odd, ODD_OUT, D_MODEL), ODD_OUT ** -0.5),
        "c_lambda_q1": nrm(ks[13], (n_odd, C_QK_DIM), 0.1),
        "c_lambda_k1": nrm(ks[14], (n_odd, C_QK_DIM), 0.1),
        "c_lambda_q2": nrm(ks[15], (n_odd, C_QK_DIM), 0.1),
        "c_lambda_k2": nrm(ks[16], (n_odd, C_QK_DIM), 0.1),
        "c_head_gain": gain(ks[17], (n_odd, C_V_DIM)),
        "w_group": nrm(ks[18], (DEPTH, D_MODEL, N_GROUPS), D_MODEL ** -0.5),
        "b_group": nrm(ks[19], (DEPTH, N_GROUPS), 0.01),
        "w_router": nrm(ks[20], (DEPTH, D_MODEL, N_EXPERTS), D_MODEL ** -0.5),
        "b_router": nrm(ks[21], (DEPTH, N_EXPERTS), 0.01),
        "w_gate": nrm(ks[22], (DEPTH, N_EXPERTS, D_MODEL, D_EXPERT), D_MODEL ** -0.5),
        "w_up": nrm(ks[23], (DEPTH, N_EXPERTS, D_MODEL, D_EXPERT), D_MODEL ** -0.5),
        "w_down": nrm(ks[24], (DEPTH, N_EXPERTS, D_EXPERT, D_MODEL), D_EXPERT ** -0.5),
        "final_gain": gain(ks[25], (D_MODEL,)),
    }


def reference(x, c, ctx, c_ctx, w_ada, b_ada, w_in_even, w_out_even, a_q_gain, a_k_gain, b_gain,
              w_in_odd, w_out_odd, c_lambda_q1, c_lambda_k1, c_lambda_q2, c_lambda_k2, c_head_gain,
              w_group, b_group, w_router, b_router, w_gate, w_up, w_down, final_gain):
    n_lat = x.shape[1]
    rope_a = axial_rope_tables(n_lat, A_HEAD_DIM)
    rope_c = axial_rope_tables(n_lat, C_QK_DIM)
    silu_c = jax.nn.silu(c)
    silu_cc = jax.nn.silu(c_ctx)
    for l in range(DEPTH):
        last = l == DEPTH - 1
        i = l // 2
        mod = silu_c @ w_ada[l] + b_ada[l]
        mod_c = silu_cc @ w_ada[l] + b_ada[l]
        sh1, sc1, g1, sh2, sc2, g2 = jnp.split(mod[:, None, :], 6, axis=-1)
        csh1, csc1, cg1, csh2, csc2, cg2 = jnp.split(mod_c, 6, axis=-1)
        h_lat = rms_norm(x) * (1 + sc1) + sh1
        h_ctx = rms_norm(ctx) * (1 + csc1) + csh1
        if l % 2 == 0:
            y_lat, y_ctx = even_mixer(h_lat, h_ctx, w_in_even[i], w_out_even[i], a_q_gain[i], a_k_gain[i],
                                      b_gain[i], rope_a, not last)
        else:
            lam_init = 0.8 - 0.6 * math.exp(-0.3 * l)
            y_lat, y_ctx = odd_mixer(h_lat, h_ctx, w_in_odd[i], w_out_odd[i], c_lambda_q1[i], c_lambda_k1[i],
                                     c_lambda_q2[i], c_lambda_k2[i], c_head_gain[i], lam_init, rope_c, not last)
        x = x + g1 * y_lat
        x = x + g2 * hierarchical_moe(rms_norm(x) * (1 + sc2) + sh2, w_group[l], b_group[l], w_router[l],
                                      b_router[l], w_gate[l], w_up[l], w_down[l])
        if not last:
            ctx = ctx + cg1 * y_ctx
            ctx = ctx + cg2 * hierarchical_moe(rms_norm(ctx) * (1 + csc2) + csh2, w_group[l], b_group[l],
                                               w_router[l], b_router[l], w_gate[l], w_up[l], w_down[l])
    return rms_norm(x, final_gain)
```

```python
import functools
import math

import numpy as np
import jax
import jax.numpy as jnp
from jax import lax
from jax.experimental import pallas as pl
from jax.experimental.pallas import tpu as pltpu

GRID_W = 64
EPS = 1e-6
ROPE_THETA = 10000.0
A_HEADS, A_KV_HEADS, A_GROUP, A_HEAD_DIM = 6, 2, 3, 128
A_Q_WIDTH, A_KV_WIDTH = 768, 256
B_GROUPS, B_GROUP_DIM, B_WIDTH = 4, 64, 256
C_HEADS, C_QK_DIM, C_V_DIM = 8, 64, 128
C_QK_WIDTH, C_V_WIDTH = 1024, 1024
N_GROUPS, EXPERTS_PER_GROUP, N_EXPERTS, D_EXPERT = 4, 4, 16, 512
N_PAIRS = 6
N_CLASSES = N_GROUPS * N_PAIRS
CLASS_ROWS = 32
PAIR_A = (0, 0, 0, 1, 1, 2)
PAIR_B = (1, 2, 3, 2, 3, 3)

LANE = 128
TM = 256
AUG = LANE
LOG2E = 1.4426950408889634
NEG_BIG = -1e30
VMEM_LIMIT = 48 * 1024 * 1024

F32 = jnp.float32
BF16 = jnp.bfloat16
NT_DIMS = (((1,), (1,)), ((), ()))


def _cparams(sem):
    return pltpu.CompilerParams(dimension_semantics=sem, vmem_limit_bytes=VMEM_LIMIT)


def _dot(a, b):
    return jnp.dot(a, b, preferred_element_type=F32)


def _dot_hi(a, b):
    return jnp.dot(a, b, preferred_element_type=F32, precision=lax.Precision.HIGHEST)


def _mod_kernel(a_ref, w_ref, b_ref, o_ref):
    a = a_ref[...]
    act = a * jax.nn.sigmoid(a)
    o_ref[...] = _dot(act.astype(BF16), w_ref[...].astype(BF16)) + b_ref[...]


def _modulation(cond, w_ada, b_ada):
    depth, d, n6 = w_ada.shape
    tn = 1024
    return pl.pallas_call(
        _mod_kernel,
        grid=(depth, n6 // tn),
        in_specs=[pl.BlockSpec((8, d), lambda l, j: (0, 0)),
                  pl.BlockSpec((None, d, tn), lambda l, j: (l, 0, j)),
                  pl.BlockSpec((None, 1, tn), lambda l, j: (l, 0, j))],
        out_specs=pl.BlockSpec((None, 8, tn), lambda l, j: (l, 0, j)),
        out_shape=jax.ShapeDtypeStruct((depth, 8, n6), F32),
        compiler_params=_cparams(("arbitrary", "arbitrary")),
        name="adaln_mod",
    )(cond, w_ada, b_ada.reshape(depth, 1, n6))


def _norm_mod(x, shift, scale):
    r = lax.rsqrt(jnp.mean(x * x, axis=-1, keepdims=True) + EPS)
    return (x * r) * (1.0 + scale) + shift


def _rope(y, cos, sin, lo, m):
    part = jnp.where(lo, pltpu.roll(y, LANE - m, 1), pltpu.roll(y, m, 1))
    return y * cos + part * sin


def _proj_even_kernel(x_ref, mod_ref, w_ref, cos_ref, sin_ref, qg_ref, kg_ref, fg_ref,
                      q_ref, k_ref, v_ref, f_ref, *, d, qscale):
    tm = x_ref.shape[0]
    h = _norm_mod(x_ref[...], mod_ref[:, 0:d], mod_ref[:, d:2 * d])
    p = _dot(h.astype(BF16), w_ref[...])
    cos, sin = cos_ref[...], sin_ref[...]
    lane = lax.broadcasted_iota(jnp.int32, (tm, LANE), 1)
    lo = (lane % (A_HEAD_DIM // 2)) < (A_HEAD_DIM // 4)

    def head(ph, gain, scale):
        ms = jnp.mean(ph * ph, axis=-1, keepdims=True)
        y = ph * lax.rsqrt(ms + EPS) * gain
        return (_rope(y, cos, sin, lo, A_HEAD_DIM // 4) * scale).astype(BF16)

    for j in range(A_HEADS):
        q_ref[:, j * LANE:(j + 1) * LANE] = head(p[:, j * LANE:(j + 1) * LANE], qg_ref[...], qscale)
    for j in range(A_KV_HEADS):
        c0 = A_Q_WIDTH + j * LANE
        k_ref[:, j * LANE:(j + 1) * LANE] = head(p[:, c0:c0 + LANE], kg_ref[...], 1.0)
    v0 = A_Q_WIDTH + A_KV_WIDTH
    v_ref[...] = p[:, v0:v0 + A_KV_WIDTH].astype(BF16)
    f = p[:, v0 + A_KV_WIDTH:v0 + A_KV_WIDTH + B_WIDTH]
    f2 = f * f
    grp = lax.broadcasted_iota(jnp.int32, (tm, B_WIDTH), 1) // B_GROUP_DIM
    ms = jnp.zeros_like(f)
    for g in range(B_GROUPS):
        mask = grp == g
        sg = jnp.sum(jnp.where(mask, f2, 0.0), axis=-1, keepdims=True) * (1.0 / B_GROUP_DIM)
        ms = jnp.where(mask, sg, ms)
    f_ref[...] = f * lax.rsqrt(ms + EPS) * fg_ref[...]


def _proj_odd_kernel(x_ref, mod_ref, w_ref, cos_ref, sin_ref, q_ref, k_ref, v_ref, *, d, qscale):
    tm = x_ref.shape[0]
    h = _norm_mod(x_ref[...], mod_ref[:, 0:d], mod_ref[:, d:2 * d])
    p = _dot(h.astype(BF16), w_ref[...])
    cos, sin = cos_ref[...], sin_ref[...]
    lane = lax.broadcasted_iota(jnp.int32, (tm, LANE), 1)
    lo = (lane % (C_QK_DIM // 2)) < (C_QK_DIM // 4)
    for j in range(C_QK_WIDTH // LANE):
        y = _rope(p[:, j * LANE:(j + 1) * LANE], cos, sin, lo, C_QK_DIM // 4)
        q_ref[:, j * LANE:(j + 1) * LANE] = (y * qscale).astype(BF16)
    for j in range(C_QK_WIDTH // LANE):
        c0 = C_QK_WIDTH + j * LANE
        y = _rope(p[:, c0:c0 + LANE], cos, sin, lo, C_QK_DIM // 4)
        k_ref[:, j * LANE:(j + 1) * LANE] = y.astype(BF16)
    v_ref[...] = p[:, 2 * C_QK_WIDTH:2 * C_QK_WIDTH + C_V_WIDTH].astype(BF16)


def _project(xs, mod4, layer, w_bf, cos, sin, gains, n_lat_tiles, even):
    bsz, t, d = xs.shape
    nt = t // TM
    mod_spec = pl.BlockSpec((None, None, 1, mod4.shape[-1]),
                            lambda b, i: (layer, jnp.where(i >= n_lat_tiles, bsz, b), 0, 0))
    tok = lambda w: pl.BlockSpec((None, TM, w), lambda b, i: (b, i, 0))
    full = lambda a: pl.BlockSpec(a.shape, lambda b, i: (0,) * a.ndim)
    tab = pl.BlockSpec((TM, LANE), lambda b, i: (i, 0))
    if even:
        qg, kg, fg = gains
        kern = functools.partial(_proj_even_kernel, d=d, qscale=A_HEAD_DIM ** -0.5 * LOG2E)
        in_specs = [tok(d), mod_spec, full(w_bf), tab, tab, full(qg), full(kg), full(fg)]
        args = (xs, mod4, w_bf, cos, sin, qg, kg, fg)
        widths = ((A_Q_WIDTH, BF16), (A_KV_WIDTH, BF16), (A_KV_WIDTH, BF16), (B_WIDTH, F32))
    else:
        kern = functools.partial(_proj_odd_kernel, d=d, qscale=C_QK_DIM ** -0.5 * LOG2E)
        in_specs = [tok(d), mod_spec, full(w_bf), tab, tab]
        args = (xs, mod4, w_bf, cos, sin)
        widths = ((C_QK_WIDTH, BF16), (C_QK_WIDTH, BF16), (C_V_WIDTH, BF16))
    return pl.pallas_call(
        kern,
        grid=(bsz, nt),
        in_specs=in_specs,
        out_specs=[tok(w) for w, _ in widths],
        out_shape=[jax.ShapeDtypeStruct((bsz, t, w), dt) for w, dt in widths],
        compiler_params=_cparams(("arbitrary", "arbitrary")),
        name="proj_even" if even else "proj_odd",
    )(*args)


def _flash_kernel(*refs, even, tq, kvt, n_kv, lam_init):
    if even:
        q_ref, k_ref, v_ref, o_ref, qs, m_sc, l_sc, acc_sc = refs
    else:
        q_ref, k_ref, v_ref, lam_ref, hg_ref, o_ref, qs, m_sc, l_sc, acc_sc = refs
    if even:
        for h in range(A_GROUP):
            qs[h * tq:(h + 1) * tq, :] = q_ref[:, h * LANE:(h + 1) * LANE]
    else:
        q = q_ref[...]
        lane = lax.broadcasted_iota(jnp.int32, q.shape, 1)
        zero = jnp.zeros_like(q)
        qs[0:tq, :] = jnp.where(lane < C_QK_DIM, q, zero)
        qs[tq:2 * tq, :] = jnp.where(lane >= C_QK_DIM, q, zero)
    m_sc[...] = jnp.full(m_sc.shape, NEG_BIG, F32)
    l_sc[...] = jnp.zeros(l_sc.shape, F32)
    acc_sc[...] = jnp.zeros(acc_sc.shape, F32)

    def body(j, carry):
        off = pl.multiple_of(j * kvt, kvt)
        k = k_ref[pl.ds(off, kvt), :]
        v = v_ref[pl.ds(off, kvt), :]
        s = lax.dot_general(qs[...], k, NT_DIMS, preferred_element_type=F32)
        m_prev = m_sc[...]
        m_new = jnp.maximum(m_prev, jnp.max(s, axis=1, keepdims=True))
        alpha = jnp.exp2(m_prev - m_new)
        p = jnp.exp2(s - m_new)
        l_sc[...] = alpha * l_sc[...] + jnp.sum(p, axis=1, keepdims=True)
        acc_sc[...] = alpha * acc_sc[...] + _dot(p.astype(BF16), v)
        m_sc[...] = m_new
        return carry

    lax.fori_loop(0, n_kv, body, 0)
    o = acc_sc[...] / l_sc[...]
    if even:
        for h in range(A_GROUP):
            o_ref[:, h * LANE:(h + 1) * LANE] = o[h * tq:(h + 1) * tq, :].astype(BF16)
    else:
        lam = (jnp.exp(jnp.sum(lam_ref[0:1, :] * lam_ref[1:2, :], axis=-1, keepdims=True))
               - jnp.exp(jnp.sum(lam_ref[2:3, :] * lam_ref[3:4, :], axis=-1, keepdims=True)) + lam_init)
        dlt = o[0:tq, :] - lam * o[tq:2 * tq, :]
        ms = jnp.mean(dlt * dlt, axis=-1, keepdims=True)
        o_ref[...] = (dlt * lax.rsqrt(ms + EPS) * hg_ref[...] * (1.0 - lam_init)).astype(BF16)


def _attention(q, k, v, *, even, q_row0, n_q, kv_row0, n_kv_rows, tq, kvt, lam=None, head_gain=None,
               lam_init=0.0, name="attn"):
    bsz = q.shape[0]
    n_grp = A_KV_HEADS if even else C_HEADS
    r_in = A_GROUP if even else 1
    r_stack = A_GROUP if even else 2
    wq = r_in * LANE
    qb0, kb0 = q_row0 // tq, kv_row0 // n_kv_rows
    assert q_row0 % tq == 0 and kv_row0 % n_kv_rows == 0 and n_kv_rows % kvt == 0 and n_q % tq == 0
    in_specs = [pl.BlockSpec((None, tq, wq), lambda b, g, i: (b, qb0 + i, g)),
                pl.BlockSpec((None, n_kv_rows, LANE), lambda b, g, i: (b, kb0, g)),
                pl.BlockSpec((None, n_kv_rows, LANE), lambda b, g, i: (b, kb0, g))]
    args = [q, k, v]
    if not even:
        in_specs += [pl.BlockSpec(lam.shape, lambda b, g, i: (0, 0)),
                     pl.BlockSpec(head_gain.shape, lambda b, g, i: (0, 0))]
        args += [lam, head_gain]
    m = r_stack * tq
    kern = functools.partial(_flash_kernel, even=even, tq=tq, kvt=kvt, n_kv=n_kv_rows // kvt,
                             lam_init=lam_init)
    return pl.pallas_call(
        kern,
        grid=(bsz, n_grp, n_q // tq),
        in_specs=in_specs,
        out_specs=pl.BlockSpec((None, tq, wq), lambda b, g, i: (b, i, g)),
        out_shape=jax.ShapeDtypeStruct((bsz, n_q, n_grp * wq), BF16),
        scratch_shapes=[pltpu.VMEM((m, LANE), BF16), pltpu.VMEM((m, 1), F32),
                        pltpu.VMEM((m, 1), F32), pltpu.VMEM((m, LANE), F32)],
        compiler_params=_cparams(("arbitrary", "arbitrary", "arbitrary")),
        name=name,
    )(*args)


def _dft_cos_sin(n, scale=1.0):
    idx = (np.arange(n)[:, None] * np.arange(n)[None, :]) % n
    ang = 2.0 * np.pi * idx / n
    return (np.cos(ang) * scale).astype(np.float32), (np.sin(ang) * scale).astype(np.float32)


def _channel_dft(width):
    c, s = _dft_cos_sin(B_GROUP_DIM)
    eye = np.eye(width // B_GROUP_DIM, dtype=np.float32)
    return np.kron(eye, c), np.kron(eye, s)


def _f1_kernel(g_ref, cc_ref, sc_ref, m1_ref, twc_ref, tws_ref, v_ref):
    n2, r = g_ref.shape[0], g_ref.shape[1]

    def body(t2, carry):
        g = g_ref[t2]
        zr = _dot_hi(g, cc_ref[...])
        zi = -_dot_hi(g, sc_ref[...])
        u = _dot_hi(m1_ref[...], jnp.concatenate([zr, zi], axis=0))
        ur, ui = u[0:r, :], u[r:2 * r, :]
        c, s = twc_ref[t2], tws_ref[t2]
        v_ref[0, t2] = ur * c + ui * s
        v_ref[1, t2] = ui * c - ur * s
        return carry

    lax.fori_loop(0, n2, body, 0)


def _f2_kernel(v_ref, fc_ref, o_ref):
    def body(k1, carry):
        o_ref[k1] = _dot_hi(fc_ref[...], v_ref[k1])
        return carry

    lax.fori_loop(0, v_ref.shape[0], body, 0)


def _fctx_kernel(g_ref, cc_ref, sc_ref, cn_ref, sn_ref, o_ref):
    g = g_ref[...]
    a = _dot_hi(g, cc_ref[...])
    b = _dot_hi(g, sc_ref[...])
    o_ref[...] = _dot_hi(cn_ref[...], a) - _dot_hi(sn_ref[...], b)


def _fourier_lat(gf_lat):
    bsz, n, w = gf_lat.shape
    n2 = GRID_W
    r = n // n2
    halves = w // LANE
    norm = 1.0 / math.sqrt(n * B_GROUP_DIM)
    cc, sc = _channel_dft(LANE)
    cr, sr = _dft_cos_sin(r)
    m1 = np.block([[cr, sr], [-sr, cr]]).astype(np.float32)
    c2, s2 = _dft_cos_sin(n2, norm)
    fc = np.concatenate([c2, s2], axis=1)
    k1 = jnp.arange(r, dtype=jnp.int32)[None, :]
    t2 = jnp.arange(n2, dtype=jnp.int32)[:, None]
    ang = ((k1 * t2) % n).astype(F32) * (2.0 * math.pi / n)
    twc = jnp.broadcast_to(jnp.cos(ang)[:, :, None], (n2, r, LANE))
    tws = jnp.broadcast_to(jnp.sin(ang)[:, :, None], (n2, r, LANE))
    g_t = gf_lat.reshape(bsz, r, n2, halves, LANE).transpose(0, 3, 2, 1, 4)
    const = lambda a: pl.BlockSpec(a.shape, lambda b, h: (0,) * a.ndim)
    v = pl.pallas_call(
        _f1_kernel,
        grid=(bsz, halves),
        in_specs=[pl.BlockSpec((None, None, n2, r, LANE), lambda b, h: (b, h, 0, 0, 0)),
                  const(cc), const(sc), const(m1), const(twc), const(tws)],
        out_specs=pl.BlockSpec((None, None, 2, n2, r, LANE), lambda b, h: (b, h, 0, 0, 0, 0)),
        out_shape=jax.ShapeDtypeStruct((bsz, halves, 2, n2, r, LANE), F32),
        compiler_params=_cparams(("arbitrary", "arbitrary")),
        name="fourier_rows",
    )(g_t, jnp.asarray(cc), jnp.asarray(sc), jnp.asarray(m1), twc, tws)
    v_k = v.transpose(0, 1, 4, 2, 3, 5).reshape(bsz, halves, r, 2 * n2, LANE)
    y = pl.pallas_call(
        _f2_kernel,
        grid=(bsz, halves),
        in_specs=[pl.BlockSpec((None, None, r, 2 * n2, LANE), lambda b, h: (b, h, 0, 0, 0)),
                  const(fc)],
        out_specs=pl.BlockSpec((None, None, r, n2, LANE), lambda b, h: (b, h, 0, 0, 0)),
        out_shape=jax.ShapeDtypeStruct((bsz, halves, r, n2, LANE), F32),
        compiler_params=_cparams(("arbitrary", "arbitrary")),
        name="fourier_cols",
    )(v_k, jnp.asarray(fc))
    return y.transpose(0, 3, 2, 1, 4).reshape(bsz, n, w)


def _fourier_ctx(gf_ctx):
    bsz, n, w = gf_ctx.shape
    cc, sc = _channel_dft(w)
    cn, sn = _dft_cos_sin(n, 1.0 / math.sqrt(n * B_GROUP_DIM))
    const = lambda a: pl.BlockSpec(a.shape, lambda b: (0,) * a.ndim)
    return pl.pallas_call(
        _fctx_kernel,
        grid=(bsz,),
        in_specs=[pl.BlockSpec((None, n, w), lambda b: (b, 0, 0)), const(cc), const(sc), const(cn), const(sn)],
        out_specs=pl.BlockSpec((None, n, w), lambda b: (b, 0, 0)),
        out_shape=jax.ShapeDtypeStruct((bsz, n, w), F32),
        compiler_params=_cparams(("arbitrary",)),
        name="fourier_ctx",
    )(gf_ctx, jnp.asarray(cc), jnp.asarray(sc), jnp.asarray(cn), jnp.asarray(sn))


def _first_max(vals):
    best = vals[0]
    for v in vals[1:]:
        best = jnp.maximum(best, v)
    idx = jnp.full(best.shape, len(vals) - 1, jnp.int32)
    for j in range(len(vals) - 2, -1, -1):
        idx = jnp.where(vals[j] == best, j, idx)
    return best, idx


def _mix_out_kernel(*refs, even, d):
    if even:
        (o_ref, yf_ref, x_ref, mod_ref, wo_ref, wrh_ref, wrl_ref, rb_ref,
         x1_ref, haug_ref, ri_ref, cnt_ref, run) = refs
    else:
        (o_ref, x_ref, mod_ref, wo_ref, wrh_ref, wrl_ref, rb_ref,
         x1_ref, haug_ref, ri_ref, cnt_ref, run) = refs
    tm = x_ref.shape[0]

    @pl.when((pl.program_id(0) == 0) & (pl.program_id(1) == 0))
    def _():
        run[...] = jnp.zeros(run.shape, F32)

    wo_rows = o_ref.shape[1]
    y = _dot(o_ref[...], wo_ref[0:wo_rows, :])
    if even:
        y = y + _dot(yf_ref[...].astype(BF16), wo_ref[wo_rows:, :])
    x1 = x_ref[...] + mod_ref[:, 2 * d:3 * d] * y
    x1_ref[...] = x1
    h2 = _norm_mod(x1, mod_ref[:, 3 * d:4 * d], mod_ref[:, 4 * d:5 * d])
    haug_ref[:, 0:d] = h2

    hi = h2.astype(BF16)
    lo = (h2 - hi.astype(F32)).astype(BF16)
    lg = (lax.dot_general(wrh_ref[...], hi, NT_DIMS, preferred_element_type=F32)
          + lax.dot_general(wrh_ref[...], lo, NT_DIMS, preferred_element_type=F32)
          + lax.dot_general(wrl_ref[...], hi, NT_DIMS, preferred_element_type=F32)) + rb_ref[...]
    g = [lg[j:j + 1, :] for j in range(N_GROUPS)]
    e = [lg[N_GROUPS + j:N_GROUPS + j + 1, :] for j in range(N_EXPERTS)]
    gmax, gidx = _first_max(g)
    gsum = jnp.exp(g[0] - gmax)
    for j in range(1, N_GROUPS):
        gsum = gsum + jnp.exp(g[j] - gmax)
    ggate = 1.0 / gsum
    esel = []
    for j in range(EXPERTS_PER_GROUP):
        v = e[(N_GROUPS - 1) * EXPERTS_PER_GROUP + j]
        for gi in range(N_GROUPS - 2, -1, -1):
            v = jnp.where(gidx == gi, e[gi * EXPERTS_PER_GROUP + j], v)
        esel.append(v)
    v1, i1 = _first_max(esel)
    rest = [jnp.where(i1 == j, -3.0e38, esel[j]) for j in range(EXPERTS_PER_GROUP)]
    v2, i2 = _first_max(rest)
    t = jnp.exp(v2 - v1)
    w1 = ggate / (1.0 + t)
    w2 = ggate * t / (1.0 + t)
    first_low = i1 < i2
    ea = jnp.where(first_low, i1, i2)
    eb = jnp.where(first_low, i2, i1)
    wa = jnp.where(first_low, w1, w2)
    wb = jnp.where(first_low, w2, w1)
    pair = jnp.where(ea == 0, 0, jnp.where(ea == 1, 3, 5)) + (eb - ea - 1)
    cls = gidx * N_PAIRS + pair

    onehot = lax.broadcasted_iota(jnp.int32, (CLASS_ROWS, tm), 0) == cls
    r_i = lax.broadcasted_iota(jnp.int32, (tm, tm), 0)
    c_i = lax.broadcasted_iota(jnp.int32, (tm, tm), 1)
    before = jnp.where(r_i < c_i, 1.0, 0.0).astype(BF16)
    oh = jnp.where(onehot, 1.0, 0.0)
    cnt = _dot(oh.astype(BF16), before) + run[:, 0:1]
    rank = jnp.sum(jnp.where(onehot, cnt, 0.0), axis=0, keepdims=True)
    run[...] = run[...] + jnp.sum(oh, axis=1, keepdims=True)
    cnt_ref[...] = run[...]
    ri_ref[...] = jnp.zeros(ri_ref.shape, jnp.int32)
    ri_ref[0:1, :] = cls
    ri_ref[1:2, :] = rank.astype(jnp.int32)

    eye = r_i == c_i
    wa_col = jnp.sum(jnp.where(eye, wa, 0.0), axis=1, keepdims=True)
    wb_col = jnp.sum(jnp.where(eye, wb, 0.0), axis=1, keepdims=True)
    lane = lax.broadcasted_iota(jnp.int32, (tm, AUG), 1)
    haug_ref[:, d:d + AUG] = jnp.where(lane == 0, wa_col, jnp.where(lane == 1, wb_col, 0.0))


def _mix_out(o, yf, xs, mod4, layer, wo_bf, wr_hi, wr_lo, rbias, n_lat_tiles, nt, even):
    bsz, t, d = xs.shape
    mod_spec = pl.BlockSpec((None, None, 1, mod4.shape[-1]),
                            lambda b, i: (layer, jnp.where(i >= n_lat_tiles, bsz, b), 0, 0))
    tok = lambda w: pl.BlockSpec((None, TM, w), lambda b, i: (b, i, 0))
    full = lambda a: pl.BlockSpec(a.shape, lambda b, i: (0,) * a.ndim)
    in_specs = [tok(o.shape[-1])] + ([tok(yf.shape[-1])] if even else []) + [
        tok(d), mod_spec, full(wo_bf), full(wr_hi), full(wr_lo), full(rbias)]
    args = [o] + ([yf] if even else []) + [xs, mod4, wo_bf, wr_hi, wr_lo, rbias]
    return pl.pallas_call(
        functools.partial(_mix_out_kernel, even=even, d=d),
        grid=(bsz, nt),
        in_specs=in_specs,
        out_specs=[tok(d), tok(d + AUG),
                   pl.BlockSpec((None, None, 8, TM), lambda b, i: (b, i, 0, 0)),
                   pl.BlockSpec((CLASS_ROWS, LANE), lambda b, i: (0, 0))],
        out_shape=[jax.ShapeDtypeStruct((bsz, nt * TM, d), F32),
                   jax.ShapeDtypeStruct((bsz, nt * TM, d + AUG), F32),
                   jax.ShapeDtypeStruct((bsz, nt, 8, TM), jnp.int32),
                   jax.ShapeDtypeStruct((CLASS_ROWS, LANE), F32)],
        scratch_shapes=[pltpu.VMEM((CLASS_ROWS, LANE), F32)],
        compiler_params=_cparams(("arbitrary", "arbitrary")),
        name="mix_out_router",
    )(*args)


def _row_copy(src, src_row, dst, dst_row, sem):
    return pltpu.make_async_copy(src.at[pl.ds(src_row, 1), :], dst.at[pl.ds(dst_row, 1), :], sem)


def _scatter_kernel(pos_ref, h_ref, xs_in, xs_out, sem):
    del xs_in
    tm = h_ref.shape[0]

    def start(j, carry):
        _row_copy(h_ref, j, xs_out, pos_ref[0, j], sem).start()
        return carry

    def wait(j, carry):
        _row_copy(h_ref, j, xs_out, pos_ref[0, j], sem).wait()
        return carry

    lax.fori_loop(0, tm, start, 0)
    lax.fori_loop(0, tm, wait, 0)


def _dispatch(haug, pos3, nt, cap):
    bsz, t, wa = haug.shape
    tiles_per_batch = t // TM
    zeros = jnp.zeros((cap, wa), F32)
    return pl.pallas_call(
        _scatter_kernel,
        grid=(bsz, nt),
        in_specs=[pl.BlockSpec((None, 1, TM), lambda b, i: (b * nt + i, 0, 0), memory_space=pltpu.SMEM),
                  pl.BlockSpec((TM, wa), lambda b, i: (b * tiles_per_batch + i, 0)),
                  pl.BlockSpec(memory_space=pl.ANY)],
        out_specs=pl.BlockSpec(memory_space=pl.ANY),
        out_shape=jax.ShapeDtypeStruct((cap, wa), F32),
        scratch_shapes=[pltpu.SemaphoreType.DMA(())],
        input_output_aliases={2: 0},
        compiler_params=_cparams(("arbitrary", "arbitrary")),
        name="moe_dispatch",
    )(pos3, haug.reshape(bsz * t, wa), zeros)


def _expert_kernel(ea_ref, eb_ref, nv_ref, x_ref, wga, wua, wda, wgb, wub, wdb, y_ref, *, d):
    del ea_ref, eb_ref
    valid = pl.program_id(0) < nv_ref[0]

    @pl.when(jnp.logical_not(valid))
    def _():
        y_ref[...] = jnp.zeros(y_ref.shape, F32)

    @pl.when(valid)
    def _():
        x = x_ref[:, 0:d].astype(BF16)
        wcol = x_ref[:, d:d + AUG]

        def ffn(wg, wu, wd):
            hg = _dot(x, wg[...])
            hu = _dot(x, wu[...])
            act = (hg * jax.nn.sigmoid(hg)) * hu
            return _dot(act.astype(BF16), wd[...])

        y_ref[...] = wcol[:, 0:1] * ffn(wga, wua, wda) + wcol[:, 1:2] * ffn(wgb, wub, wdb)


def _experts(xsort, tile_ea, tile_eb, nvalid, wg, wu, wd):
    cap, wa = xsort.shape
    d = wa - AUG
    n_tiles = cap // TM
    row = lambda i, ea, eb, nv: (jnp.minimum(i, nv[0] - 1), 0)
    wspec = lambda shape, sel: pl.BlockSpec((None,) + shape, lambda i, ea, eb, nv: ((ea, eb)[sel][i], 0, 0))
    grid_spec = pltpu.PrefetchScalarGridSpec(
        num_scalar_prefetch=3,
        grid=(n_tiles,),
        in_specs=[pl.BlockSpec((TM, wa), row),
                  wspec((d, D_EXPERT), 0), wspec((d, D_EXPERT), 0), wspec((D_EXPERT, d), 0),
                  wspec((d, D_EXPERT), 1), wspec((d, D_EXPERT), 1), wspec((D_EXPERT, d), 1)],
        out_specs=pl.BlockSpec((TM, d), lambda i, ea, eb, nv: (i, 0)),
    )
    return pl.pallas_call(
        functools.partial(_expert_kernel, d=d),
        grid_spec=grid_spec,
        out_shape=jax.ShapeDtypeStruct((cap, d), F32),
        compiler_params=_cparams(("arbitrary",)),
        name="moe_experts",
    )(tile_ea, tile_eb, nvalid, xsort, wg, wu, wd, wg, wu, wd)


def _combine_kernel(pos_ref, y_hbm, x1_ref, mod_ref, o_ref, buf, sem, *, d):
    tm = x1_ref.shape[0]

    def start(j, carry):
        _row_copy(y_hbm, pos_ref[0, j], buf, j, sem).start()
        return carry

    def wait(j, carry):
        _row_copy(y_hbm, pos_ref[0, j], buf, j, sem).wait()
        return carry

    lax.fori_loop(0, tm, start, 0)
    lax.fori_loop(0, tm, wait, 0)
    o_ref[...] = x1_ref[...] + mod_ref[:, 5 * d:6 * d] * buf[...]


def _combine(ysort, pos3, x1, mod4, layer, n_lat_tiles, nt):
    bsz, t, d = x1.shape
    mod_spec = pl.BlockSpec((None, None, 1, mod4.shape[-1]),
                            lambda b, i: (layer, jnp.where(i >= n_lat_tiles, bsz, b), 0, 0))
    return pl.pallas_call(
        functools.partial(_combine_kernel, d=d),
        grid=(bsz, nt),
        in_specs=[pl.BlockSpec((None, 1, TM), lambda b, i: (b * nt + i, 0, 0), memory_space=pltpu.SMEM),
                  pl.BlockSpec(memory_space=pl.ANY),
                  pl.BlockSpec((None, TM, d), lambda b, i: (b, i, 0)),
                  mod_spec],
        out_specs=pl.BlockSpec((None, TM, d), lambda b, i: (b, i, 0)),
        out_shape=jax.ShapeDtypeStruct((bsz, t, d), F32),
        scratch_shapes=[pltpu.VMEM((TM, d), F32), pltpu.SemaphoreType.DMA(())],
        compiler_params=_cparams(("arbitrary", "arbitrary")),
        name="moe_combine",
    )(pos3, ysort, x1, mod4)


def _moe(haug, ri, counts, x1, mod4, layer, wg, wu, wd, n_lat_tiles, nt):
    bsz = haug.shape[0]
    n_tok = bsz * nt * TM
    n_tiles = n_tok // TM + N_CLASSES
    cls = ri[:, :, 0, :].reshape(-1)
    rank = ri[:, :, 1, :].reshape(-1)
    cnt = counts[:N_CLASSES, 0].astype(jnp.int32)
    tiles_per = (cnt + TM - 1) // TM
    tile_end = jnp.cumsum(tiles_per)
    row_off = (tile_end - tiles_per) * TM
    nvalid = tile_end[-1:]
    tile_id = jnp.minimum(jnp.arange(n_tiles, dtype=jnp.int32), nvalid - 1)
    tile_cls = jnp.sum((tile_id[:, None] >= tile_end[None, :]).astype(jnp.int32), axis=1)
    grp, pair = tile_cls // N_PAIRS, tile_cls % N_PAIRS
    tile_ea = grp * EXPERTS_PER_GROUP + jnp.asarray(PAIR_A, jnp.int32)[pair]
    tile_eb = grp * EXPERTS_PER_GROUP + jnp.asarray(PAIR_B, jnp.int32)[pair]
    pos3 = (row_off[cls] + rank).reshape(bsz * nt, 1, TM)
    xsort = _dispatch(haug, pos3, nt, n_tiles * TM)
    ysort = _experts(xsort, tile_ea, tile_eb, nvalid.astype(jnp.int32), wg, wu, wd)
    return _combine(ysort, pos3, x1, mod4, layer, n_lat_tiles, nt)


def _final_kernel(x_ref, g_ref, o_ref):
    x = x_ref[...]
    o_ref[...] = x * lax.rsqrt(jnp.mean(x * x, axis=-1, keepdims=True) + EPS) * g_ref[...]


def _final_norm(xs, gain, n_lat):
    bsz, _, d = xs.shape
    return pl.pallas_call(
        _final_kernel,
        grid=(bsz, n_lat // TM),
        in_specs=[pl.BlockSpec((None, TM, d), lambda b, i: (b, i, 0)),
                  pl.BlockSpec((1, d), lambda b, i: (0, 0))],
        out_specs=pl.BlockSpec((None, TM, d), lambda b, i: (b, i, 0)),
        out_shape=jax.ShapeDtypeStruct((bsz, n_lat, d), F32),
        compiler_params=_cparams(("arbitrary", "arbitrary")),
        name="final_norm",
    )(xs, gain.reshape(1, d))


def _rope_tables(n_lat, n_ctx, dim):
    rows = n_lat // GRID_W
    row = jnp.broadcast_to(jnp.arange(rows, dtype=F32)[:, None], (rows, GRID_W)).reshape(-1)
    col = jnp.broadcast_to(jnp.arange(GRID_W, dtype=F32)[None, :], (rows, GRID_W)).reshape(-1)
    half = dim // 2
    inv_freq = ROPE_THETA ** (-jnp.arange(0, half, 2, dtype=F32) / half)
    ang_r, ang_c = row[:, None] * inv_freq, col[:, None] * inv_freq
    cr, sr, cc, sc = jnp.cos(ang_r), jnp.sin(ang_r), jnp.cos(ang_c), jnp.sin(ang_c)
    cos = jnp.tile(jnp.concatenate([cr, cr, cc, cc], axis=-1), (1, LANE // dim))
    sin = jnp.tile(jnp.concatenate([-sr, sr, -sc, sc], axis=-1), (1, LANE // dim))
    cos = jnp.concatenate([cos, jnp.ones((n_ctx, LANE), F32)], axis=0)
    sin = jnp.concatenate([sin, jnp.zeros((n_ctx, LANE), F32)], axis=0)
    return cos, sin


def _split_bf16(w):
    hi = w.astype(BF16)
    return hi, (w - hi.astype(F32)).astype(BF16)


def kernel(x, c, ctx, c_ctx, w_ada, b_ada, w_in_even, w_out_even, a_q_gain, a_k_gain, b_gain, w_in_odd, w_out_odd, c_lambda_q1, c_lambda_k1, c_lambda_q2, c_lambda_k2, c_head_gain, w_group, b_group, w_router, b_router, w_gate, w_up, w_down, final_gain):
    bsz, n_lat, d = x.shape
    n_ctx = ctx.shape[1]
    depth = w_ada.shape[0]
    assert n_lat % TM == 0 and n_ctx == TM and bsz < 8 and d % LANE == 0
    n_lat_tiles = n_lat // TM
    t = n_lat + n_ctx

    xs = jnp.concatenate([x, ctx], axis=1)
    cond = jnp.zeros((8, d), F32).at[:bsz].set(c).at[bsz].set(c_ctx)
    mod = _modulation(cond, w_ada, b_ada)
    mod4 = mod.reshape(depth, 8, 1, mod.shape[-1])
    rope_a = _rope_tables(n_lat, n_ctx, A_HEAD_DIM)
    rope_c = _rope_tables(n_lat, n_ctx, C_QK_DIM)

    for l in range(depth):
        last = l == depth - 1
        i = l // 2
        even = l % 2 == 0
        nt = n_lat_tiles if last else n_lat_tiles + 1
        if even:
            gains = (a_q_gain[i].reshape(1, -1), a_k_gain[i].reshape(1, -1), b_gain[i].reshape(1, -1))
            q, k, v, gf = _project(xs, mod4, l, w_in_even[i].astype(BF16), *rope_a, gains, n_lat_tiles, True)
            o = _attention(q, k, v, even=True, q_row0=0, n_q=n_lat, kv_row0=0, n_kv_rows=t,
                           tq=256, kvt=768, name="attn_even_lat")
            yf = _fourier_lat(gf[:, :n_lat])
            if not last:
                o_c = _attention(q, k, v, even=True, q_row0=n_lat, n_q=n_ctx, kv_row0=n_lat, n_kv_rows=n_ctx,
                                 tq=n_ctx, kvt=n_ctx, name="attn_even_ctx")
                o = jnp.concatenate([o, o_c], axis=1)
                yf = jnp.concatenate([yf, _fourier_ctx(gf[:, n_lat:])], axis=1)
            w_out = w_out_even[i]
        else:
            lam_init = 0.8 - 0.6 * math.exp(-0.3 * l)
            lam = jnp.stack([c_lambda_q1[i], c_lambda_k1[i], c_lambda_q2[i], c_lambda_k2[i]])
            hg = c_head_gain[i].reshape(1, -1)
            q, k, v = _project(xs, mod4, l, w_in_odd[i].astype(BF16), *rope_c, None, n_lat_tiles, False)
            o = _attention(q, k, v, even=False, q_row0=0, n_q=n_lat, kv_row0=0, n_kv_rows=t,
                           tq=512, kvt=768, lam=lam, head_gain=hg, lam_init=lam_init, name="attn_odd_lat")
            yf = None
            if not last:
                o_c = _attention(q, k, v, even=False, q_row0=n_lat, n_q=n_ctx, kv_row0=n_lat, n_kv_rows=n_ctx,
                                 tq=n_ctx, kvt=n_ctx, lam=lam, head_gain=hg, lam_init=lam_init,
                                 name="attn_odd_ctx")
                o = jnp.concatenate([o, o_c], axis=1)
            w_out = w_out_odd[i]
        w_r = jnp.zeros((CLASS_ROWS, d), F32).at[:N_GROUPS].set(w_group[l].T).at[
            N_GROUPS:N_GROUPS + N_EXPERTS].set(w_router[l].T)
        b_r = jnp.zeros((CLASS_ROWS,), F32).at[:N_GROUPS].set(b_group[l]).at[
            N_GROUPS:N_GROUPS + N_EXPERTS].set(b_router[l])
        wr_hi, wr_lo = _split_bf16(w_r)
        rbias = jnp.broadcast_to(b_r[:, None], (CLASS_ROWS, TM))
        x1, haug, ri, counts = _mix_out(o, yf, xs, mod4, l, w_out.astype(BF16), wr_hi, wr_lo, rbias,
                                        n_lat_tiles, nt, even)
        xs = _moe(haug, ri, counts, x1, mod4, l, w_gate[l].astype(BF16), w_up[l].astype(BF16),
                  w_down[l].astype(BF16), n_lat_tiles, nt)
    return _final_norm(xs, final_gain, n_lat)
```

```python
import functools
import math

import numpy as np
import jax
import jax.numpy as jnp
from jax import lax
from jax.experimental import pallas as pl
from jax.experimental.pallas import tpu as pltpu

GRID_W = 64
EPS = 1e-6
ROPE_THETA = 10000.0
A_HEADS, A_KV_HEADS, A_GROUP, A_HEAD_DIM = 6, 2, 3, 128
A_Q_WIDTH, A_KV_WIDTH = 768, 256
B_GROUPS, B_GROUP_DIM, B_WIDTH = 4, 64, 256
C_HEADS, C_QK_DIM, C_V_DIM = 8, 64, 128
C_QK_WIDTH, C_V_WIDTH = 1024, 1024
N_GROUPS, EXPERTS_PER_GROUP, N_EXPERTS, D_EXPERT = 4, 4, 16, 512
N_PAIRS = 6
N_CLASSES = N_GROUPS * N_PAIRS
CLASS_ROWS = 32
PAIR_A = (0, 0, 0, 1, 1, 2)
PAIR_B = (1, 2, 3, 2, 3, 3)

LANE = 128
TM = 256
AUG = LANE
LOG2E = 1.4426950408889634
NEG_BIG = -1e30
VMEM_LIMIT = 48 * 1024 * 1024

F32 = jnp.float32
BF16 = jnp.bfloat16
NT_DIMS = (((1,), (1,)), ((), ()))


def _cparams(sem):
    return pltpu.CompilerParams(dimension_semantics=sem, vmem_limit_bytes=VMEM_LIMIT)


def _dot(a, b):
    return jnp.dot(a, b, preferred_element_type=F32)


def _dot_hi(a, b):
    return jnp.dot(a, b, preferred_element_type=F32, precision=lax.Precision.HIGHEST)


def _mod_kernel(a_ref, w_ref, b_ref, o_ref):
    a = a_ref[...]
    act = a * jax.nn.sigmoid(a)
    o_ref[...] = _dot(act.astype(BF16), w_ref[...].astype(BF16)) + b_ref[...]


def _modulation(cond, w_ada, b_ada):
    depth, d, n6 = w_ada.shape
    tn = 1024
    return pl.pallas_call(
        _mod_kernel,
        grid=(depth, n6 // tn),
        in_specs=[pl.BlockSpec((8, d), lambda l, j: (0, 0)),
                  pl.BlockSpec((None, d, tn), lambda l, j: (l, 0, j)),
                  pl.BlockSpec((None, 1, tn), lambda l, j: (l, 0, j))],
        out_specs=pl.BlockSpec((None, 8, tn), lambda l, j: (l, 0, j)),
        out_shape=jax.ShapeDtypeStruct((depth, 8, n6), F32),
        compiler_params=_cparams(("arbitrary", "arbitrary")),
        name="adaln_mod",
    )(cond, w_ada, b_ada.reshape(depth, 1, n6))


def _norm_mod(x, shift, scale):
    r = lax.rsqrt(jnp.mean(x * x, axis=-1, keepdims=True) + EPS)
    return (x * r) * (1.0 + scale) + shift


def _rope(y, cos, sin, lo, m):
    part = jnp.where(lo, pltpu.roll(y, LANE - m, 1), pltpu.roll(y, m, 1))
    return y * cos + part * sin


def _proj_even_kernel(x_ref, mod_ref, w_ref, cos_ref, sin_ref, qg_ref, kg_ref, fg_ref,
                      q_ref, k_ref, v_ref, f_ref, *, d, qscale):
    tm = x_ref.shape[0]
    h = _norm_mod(x_ref[...], mod_ref[:, 0:d], mod_ref[:, d:2 * d])
    p = _dot(h.astype(BF16), w_ref[...])
    cos, sin = cos_ref[...], sin_ref[...]
    lane = lax.broadcasted_iota(jnp.int32, (tm, LANE), 1)
    lo = (lane % (A_HEAD_DIM // 2)) < (A_HEAD_DIM // 4)

    def head(ph, gain, scale):
        ms = jnp.mean(ph * ph, axis=-1, keepdims=True)
        y = ph * lax.rsqrt(ms + EPS) * gain
        return (_rope(y, cos, sin, lo, A_HEAD_DIM // 4) * scale).astype(BF16)

    for j in range(A_HEADS):
        q_ref[:, j * LANE:(j + 1) * LANE] = head(p[:, j * LANE:(j + 1) * LANE], qg_ref[...], qscale)
    for j in range(A_KV_HEADS):
        c0 = A_Q_WIDTH + j * LANE
        k_ref[:, j * LANE:(j + 1) * LANE] = head(p[:, c0:c0 + LANE], kg_ref[...], 1.0)
    v0 = A_Q_WIDTH + A_KV_WIDTH
    v_ref[...] = p[:, v0:v0 + A_KV_WIDTH].astype(BF16)
    f = p[:, v0 + A_KV_WIDTH:v0 + A_KV_WIDTH + B_WIDTH]
    f2 = f * f
    grp = lax.broadcasted_iota(jnp.int32, (tm, B_WIDTH), 1) // B_GROUP_DIM
    ms = jnp.zeros_like(f)
    for g in range(B_GROUPS):
        mask = grp == g
        sg = jnp.sum(jnp.where(mask, f2, 0.0), axis=-1, keepdims=True) * (1.0 / B_GROUP_DIM)
        ms = jnp.where(mask, sg, ms)
    f_ref[...] = f * lax.rsqrt(ms + EPS) * fg_ref[...]


def _proj_odd_kernel(x_ref, mod_ref, w_ref, cos_ref, sin_ref, q_ref, k_ref, v_ref, *, d, qscale):
    tm = x_ref.shape[0]
    h = _norm_mod(x_ref[...], mod_ref[:, 0:d], mod_ref[:, d:2 * d])
    p = _dot(h.astype(BF16), w_ref[...])
    cos, sin = cos_ref[...], sin_ref[...]
    lane = lax.broadcasted_iota(jnp.int32, (tm, LANE), 1)
    lo = (lane % (C_QK_DIM // 2)) < (C_QK_DIM // 4)
    for j in range(C_QK_WIDTH // LANE):
        y = _rope(p[:, j * LANE:(j + 1) * LANE], cos, sin, lo, C_QK_DIM // 4)
        q_ref[:, j * LANE:(j + 1) * LANE] = (y * qscale).astype(BF16)
    for j in range(C_QK_WIDTH // LANE):
        c0 = C_QK_WIDTH + j * LANE
        y = _rope(p[:, c0:c0 + LANE], cos, sin, lo, C_QK_DIM // 4)
        k_ref[:, j * LANE:(j + 1) * LANE] = y.astype(BF16)
    v_ref[...] = p[:, 2 * C_QK_WIDTH:2 * C_QK_WIDTH + C_V_WIDTH].astype(BF16)


def _project(xs, mod4, layer, w_bf, cos, sin, gains, n_lat_tiles, even):
    bsz, t, d = xs.shape
    nt = t // TM
    mod_spec = pl.BlockSpec((None, None, 1, mod4.shape[-1]),
                            lambda b, i: (layer, jnp.where(i >= n_lat_tiles, bsz, b), 0, 0))
    tok = lambda w: pl.BlockSpec((None, TM, w), lambda b, i: (b, i, 0))
    full = lambda a: pl.BlockSpec(a.shape, lambda b, i: (0,) * a.ndim)
    tab = pl.BlockSpec((TM, LANE), lambda b, i: (i, 0))
    if even:
        qg, kg, fg = gains
        kern = functools.partial(_proj_even_kernel, d=d, qscale=A_HEAD_DIM ** -0.5 * LOG2E)
        in_specs = [tok(d), mod_spec, full(w_bf), tab, tab, full(qg), full(kg), full(fg)]
        args = (xs, mod4, w_bf, cos, sin, qg, kg, fg)
        widths = ((A_Q_WIDTH, BF16), (A_KV_WIDTH, BF16), (A_KV_WIDTH, BF16), (B_WIDTH, F32))
    else:
        kern = functools.partial(_proj_odd_kernel, d=d, qscale=C_QK_DIM ** -0.5 * LOG2E)
        in_specs = [tok(d), mod_spec, full(w_bf), tab, tab]
        args = (xs, mod4, w_bf, cos, sin)
        widths = ((C_QK_WIDTH, BF16), (C_QK_WIDTH, BF16), (C_V_WIDTH, BF16))
    return pl.pallas_call(
        kern,
        grid=(bsz, nt),
        in_specs=in_specs,
        out_specs=[tok(w) for w, _ in widths],
        out_shape=[jax.ShapeDtypeStruct((bsz, t, w), dt) for w, dt in widths],
        compiler_params=_cparams(("arbitrary", "arbitrary")),
        name="proj_even" if even else "proj_odd",
    )(*args)


def _flash_kernel(*refs, even, tq, kvt, n_kv, lam_init):
    if even:
        qt_ref, k_ref, vt_ref, o_ref, qcat, m_sc, l_sc, acc_sc, s_a, s_b = refs
    else:
        qt_ref, k_ref, vt_ref, lam_ref, hg_ref, o_ref, qcat, m_sc, l_sc, acc_sc, s_a, s_b = refs
    if even:
        for h in range(A_GROUP):
            qcat[:, h * tq:(h + 1) * tq] = qt_ref[h * LANE:(h + 1) * LANE, :]
    else:
        qt = qt_ref[...]
        row = lax.broadcasted_iota(jnp.int32, qt.shape, 0)
        zero = jnp.zeros_like(qt)
        qcat[:, 0:tq] = jnp.where(row < C_QK_DIM, qt, zero)
        qcat[:, tq:2 * tq] = jnp.where(row >= C_QK_DIM, qt, zero)
    m_sc[...] = jnp.full(m_sc.shape, NEG_BIG, F32)
    l_sc[...] = jnp.zeros(l_sc.shape, F32)
    acc_sc[...] = jnp.zeros(acc_sc.shape, F32)

    def scores(j, dst):
        off = pl.multiple_of(j * kvt, kvt)
        dst[...] = _dot(k_ref[pl.ds(off, kvt), :], qcat[...])

    def update(j, src):
        st = src[...]
        m_prev = m_sc[...]
        m_new = jnp.maximum(m_prev, jnp.max(st, axis=0, keepdims=True))
        alpha = jnp.exp2(m_prev - m_new)
        p = jnp.exp2(st - m_new)
        l_sc[...] = alpha * l_sc[...] + jnp.sum(p, axis=0, keepdims=True)
        acc_sc[...] = alpha * acc_sc[...] + _dot(vt_ref[j], p.astype(BF16))
        m_sc[...] = m_new

    def pair(i, carry):
        scores(2 * i + 1, s_b)
        update(2 * i, s_a)
        scores(2 * i + 2, s_a)
        update(2 * i + 1, s_b)
        return carry

    scores(0, s_a)
    n_pairs = (n_kv - 1) // 2
    if n_pairs > 0:
        lax.fori_loop(0, n_pairs, pair, 0)
    if (n_kv - 1) % 2 == 1:
        scores(n_kv - 1, s_b)
        update(n_kv - 2, s_a)
        update(n_kv - 1, s_b)
    else:
        update(n_kv - 1, s_a)
    ot = acc_sc[...] / l_sc[...]
    if even:
        for h in range(A_GROUP):
            o_ref[:, h * LANE:(h + 1) * LANE] = ot[:, h * tq:(h + 1) * tq].T.astype(BF16)
    else:
        lam = (jnp.exp(jnp.sum(lam_ref[0:1, :] * lam_ref[1:2, :], axis=-1, keepdims=True))
               - jnp.exp(jnp.sum(lam_ref[2:3, :] * lam_ref[3:4, :], axis=-1, keepdims=True)) + lam_init)
        dlt = ot[:, 0:tq] - lam * ot[:, tq:2 * tq]
        ms = jnp.mean(dlt * dlt, axis=0, keepdims=True)
        y = (dlt * lax.rsqrt(ms + EPS)).T
        o_ref[...] = (y * hg_ref[...] * (1.0 - lam_init)).astype(BF16)


def _attention(qt, k, v, *, even, q_row0, n_q, kv_row0, n_kv_rows, tq, kvt, lam=None, head_gain=None,
               lam_init=0.0, name="attn"):
    bsz = qt.shape[0]
    n_grp = A_KV_HEADS if even else C_HEADS
    r_in = A_GROUP if even else 1
    r_stack = A_GROUP if even else 2
    wq = r_in * LANE
    n_kv = n_kv_rows // kvt
    qb0, kb0 = q_row0 // tq, kv_row0 // n_kv_rows
    assert q_row0 % tq == 0 and kv_row0 % n_kv_rows == 0 and n_kv_rows % kvt == 0 and n_q % tq == 0
    vt = v[:, kv_row0:kv_row0 + n_kv_rows].reshape(bsz, n_kv, kvt, n_grp, LANE).transpose(0, 3, 1, 4, 2)
    in_specs = [pl.BlockSpec((None, wq, tq), lambda b, g, i: (b, g, qb0 + i)),
                pl.BlockSpec((None, n_kv_rows, LANE), lambda b, g, i: (b, kb0, g)),
                pl.BlockSpec((None, None, n_kv, LANE, kvt), lambda b, g, i: (b, g, 0, 0, 0))]
    args = [qt, k, vt]
    if not even:
        in_specs += [pl.BlockSpec(lam.shape, lambda b, g, i: (0, 0)),
                     pl.BlockSpec(head_gain.shape, lambda b, g, i: (0, 0))]
        args += [lam, head_gain]
    mq = r_stack * tq
    kern = functools.partial(_flash_kernel, even=even, tq=tq, kvt=kvt, n_kv=n_kv, lam_init=lam_init)
    return pl.pallas_call(
        kern,
        grid=(bsz, n_grp, n_q // tq),
        in_specs=in_specs,
        out_specs=pl.BlockSpec((None, tq, wq), lambda b, g, i: (b, i, g)),
        out_shape=jax.ShapeDtypeStruct((bsz, n_q, n_grp * wq), BF16),
        scratch_shapes=[pltpu.VMEM((LANE, mq), BF16), pltpu.VMEM((1, mq), F32),
                        pltpu.VMEM((1, mq), F32), pltpu.VMEM((LANE, mq), F32),
                        pltpu.VMEM((kvt, mq), F32), pltpu.VMEM((kvt, mq), F32)],
        compiler_params=_cparams(("arbitrary", "arbitrary", "arbitrary")),
        name=name,
    )(*args)


def _dft_cos_sin(n, scale=1.0):
    idx = (np.arange(n)[:, None] * np.arange(n)[None, :]) % n
    ang = 2.0 * np.pi * idx / n
    return (np.cos(ang) * scale).astype(np.float32), (np.sin(ang) * scale).astype(np.float32)


def _channel_dft(width):
    c, s = _dft_cos_sin(B_GROUP_DIM)
    eye = np.eye(width // B_GROUP_DIM, dtype=np.float32)
    return np.kron(eye, c), np.kron(eye, s)


def _f1_kernel(g_ref, cc_ref, sc_ref, m1_ref, twc_ref, tws_ref, v_ref):
    n2, r = g_ref.shape[0], g_ref.shape[1]

    def body(t2, carry):
        g = g_ref[t2]
        zr = _dot_hi(g, cc_ref[...])
        zi = -_dot_hi(g, sc_ref[...])
        u = _dot_hi(m1_ref[...], jnp.concatenate([zr, zi], axis=0))
        ur, ui = u[0:r, :], u[r:2 * r, :]
        c, s = twc_ref[t2], tws_ref[t2]
        v_ref[0, t2] = ur * c + ui * s
        v_ref[1, t2] = ui * c - ur * s
        return carry

    lax.fori_loop(0, n2, body, 0)


def _f2_kernel(v_ref, fc_ref, o_ref):
    def body(k1, carry):
        o_ref[k1] = _dot_hi(fc_ref[...], v_ref[k1])
        return carry

    lax.fori_loop(0, v_ref.shape[0], body, 0)


def _fctx_kernel(g_ref, cc_ref, sc_ref, cn_ref, sn_ref, o_ref):
    g = g_ref[...]
    a = _dot_hi(g, cc_ref[...])
    b = _dot_hi(g, sc_ref[...])
    o_ref[...] = _dot_hi(cn_ref[...], a) - _dot_hi(sn_ref[...], b)


def _fourier_lat(gf_lat):
    bsz, n, w = gf_lat.shape
    n2 = GRID_W
    r = n // n2
    halves = w // LANE
    norm = 1.0 / math.sqrt(n * B_GROUP_DIM)
    cc, sc = _channel_dft(LANE)
    cr, sr = _dft_cos_sin(r)
    m1 = np.block([[cr, sr], [-sr, cr]]).astype(np.float32)
    c2, s2 = _dft_cos_sin(n2, norm)
    fc = np.concatenate([c2, s2], axis=1)
    k1 = jnp.arange(r, dtype=jnp.int32)[None, :]
    t2 = jnp.arange(n2, dtype=jnp.int32)[:, None]
    ang = ((k1 * t2) % n).astype(F32) * (2.0 * math.pi / n)
    twc = jnp.broadcast_to(jnp.cos(ang)[:, :, None], (n2, r, LANE))
    tws = jnp.broadcast_to(jnp.sin(ang)[:, :, None], (n2, r, LANE))
    g_t = gf_lat.reshape(bsz, r, n2, halves, LANE).transpose(0, 3, 2, 1, 4)
    const = lambda a: pl.BlockSpec(a.shape, lambda b, h: (0,) * a.ndim)
    v = pl.pallas_call(
        _f1_kernel,
        grid=(bsz, halves),
        in_specs=[pl.BlockSpec((None, None, n2, r, LANE), lambda b, h: (b, h, 0, 0, 0)),
                  const(cc), const(sc), const(m1), const(twc), const(tws)],
        out_specs=pl.BlockSpec((None, None, 2, n2, r, LANE), lambda b, h: (b, h, 0, 0, 0, 0)),
        out_shape=jax.ShapeDtypeStruct((bsz, halves, 2, n2, r, LANE), F32),
        compiler_params=_cparams(("arbitrary", "arbitrary")),
        name="fourier_rows",
    )(g_t, jnp.asarray(cc), jnp.asarray(sc), jnp.asarray(m1), twc, tws)
    v_k = v.transpose(0, 1, 4, 2, 3, 5).reshape(bsz, halves, r, 2 * n2, LANE)
    y = pl.pallas_call(
        _f2_kernel,
        grid=(bsz, halves),
        in_specs=[pl.BlockSpec((None, None, r, 2 * n2, LANE), lambda b, h: (b, h, 0, 0, 0)),
                  const(fc)],
        out_specs=pl.BlockSpec((None, None, r, n2, LANE), lambda b, h: (b, h, 0, 0, 0)),
        out_shape=jax.ShapeDtypeStruct((bsz, halves, r, n2, LANE), F32),
        compiler_params=_cparams(("arbitrary", "arbitrary")),
        name="fourier_cols",
    )(v_k, jnp.asarray(fc))
    return y.transpose(0, 3, 2, 1, 4).reshape(bsz, n, w)


def _fourier_ctx(gf_ctx):
    bsz, n, w = gf_ctx.shape
    cc, sc = _channel_dft(w)
    cn, sn = _dft_cos_sin(n, 1.0 / math.sqrt(n * B_GROUP_DIM))
    const = lambda a: pl.BlockSpec(a.shape, lambda b: (0,) * a.ndim)
    return pl.pallas_call(
        _fctx_kernel,
        grid=(bsz,),
        in_specs=[pl.BlockSpec((None, n, w), lambda b: (b, 0, 0)), const(cc), const(sc), const(cn), const(sn)],
        out_specs=pl.BlockSpec((None, n, w), lambda b: (b, 0, 0)),
        out_shape=jax.ShapeDtypeStruct((bsz, n, w), F32),
        compiler_params=_cparams(("arbitrary",)),
        name="fourier_ctx",
    )(gf_ctx, jnp.asarray(cc), jnp.asarray(sc), jnp.asarray(cn), jnp.asarray(sn))


def _first_max(vals):
    best = vals[0]
    for v in vals[1:]:
        best = jnp.maximum(best, v)
    idx = jnp.full(best.shape, len(vals) - 1, jnp.int32)
    for j in range(len(vals) - 2, -1, -1):
        idx = jnp.where(vals[j] == best, j, idx)
    return best, idx


def _mix_out_kernel(*refs, even, d):
    if even:
        (o_ref, yf_ref, x_ref, mod_ref, wo_ref, wrh_ref, wrl_ref, rb_ref,
         x1_ref, haug_ref, ri_ref, cnt_ref, run) = refs
    else:
        (o_ref, x_ref, mod_ref, wo_ref, wrh_ref, wrl_ref, rb_ref,
         x1_ref, haug_ref, ri_ref, cnt_ref, run) = refs
    tm = x_ref.shape[0]

    @pl.when((pl.program_id(0) == 0) & (pl.program_id(1) == 0))
    def _():
        run[...] = jnp.zeros(run.shape, F32)

    wo_rows = o_ref.shape[1]
    y = _dot(o_ref[...], wo_ref[0:wo_rows, :])
    if even:
        y = y + _dot(yf_ref[...].astype(BF16), wo_ref[wo_rows:, :])
    x1 = x_ref[...] + mod_ref[:, 2 * d:3 * d] * y
    x1_ref[...] = x1
    h2 = _norm_mod(x1, mod_ref[:, 3 * d:4 * d], mod_ref[:, 4 * d:5 * d])
    haug_ref[:, 0:d] = h2

    hi = h2.astype(BF16)
    lo = (h2 - hi.astype(F32)).astype(BF16)
    lg = (lax.dot_general(wrh_ref[...], hi, NT_DIMS, preferred_element_type=F32)
          + lax.dot_general(wrh_ref[...], lo, NT_DIMS, preferred_element_type=F32)
          + lax.dot_general(wrl_ref[...], hi, NT_DIMS, preferred_element_type=F32)) + rb_ref[...]
    g = [lg[j:j + 1, :] for j in range(N_GROUPS)]
    e = [lg[N_GROUPS + j:N_GROUPS + j + 1, :] for j in range(N_EXPERTS)]
    gmax, gidx = _first_max(g)
    gsum = jnp.exp(g[0] - gmax)
    for j in range(1, N_GROUPS):
        gsum = gsum + jnp.exp(g[j] - gmax)
    ggate = 1.0 / gsum
    esel = []
    for j in range(EXPERTS_PER_GROUP):
        v = e[(N_GROUPS - 1) * EXPERTS_PER_GROUP + j]
        for gi in range(N_GROUPS - 2, -1, -1):
            v = jnp.where(gidx == gi, e[gi * EXPERTS_PER_GROUP + j], v)
        esel.append(v)
    v1, i1 = _first_max(esel)
    rest = [jnp.where(i1 == j, -3.0e38, esel[j]) for j in range(EXPERTS_PER_GROUP)]
    v2, i2 = _first_max(rest)
    t = jnp.exp(v2 - v1)
    w1 = ggate / (1.0 + t)
    w2 = ggate * t / (1.0 + t)
    first_low = i1 < i2
    ea = jnp.where(first_low, i1, i2)
    eb = jnp.where(first_low, i2, i1)
    wa = jnp.where(first_low, w1, w2)
    wb = jnp.where(first_low, w2, w1)
    pair = jnp.where(ea == 0, 0, jnp.where(ea == 1, 3, 5)) + (eb - ea - 1)
    cls = gidx * N_PAIRS + pair

    onehot = lax.broadcasted_iota(jnp.int32, (CLASS_ROWS, tm), 0) == cls
    r_i = lax.broadcasted_iota(jnp.int32, (tm, tm), 0)
    c_i = lax.broadcasted_iota(jnp.int32, (tm, tm), 1)
    before = jnp.where(r_i < c_i, 1.0, 0.0).astype(BF16)
    oh = jnp.where(onehot, 1.0, 0.0)
    cnt = _dot(oh.astype(BF16), before) + run[:, 0:1]
    rank = jnp.sum(jnp.where(onehot, cnt, 0.0), axis=0, keepdims=True)
    run[...] = run[...] + jnp.sum(oh, axis=1, keepdims=True)
    cnt_ref[...] = run[...]
    ri_ref[...] = jnp.zeros(ri_ref.shape, jnp.int32)
    ri_ref[0:1, :] = cls
    ri_ref[1:2, :] = rank.astype(jnp.int32)

    eye = r_i == c_i
    wa_col = jnp.sum(jnp.where(eye, wa, 0.0), axis=1, keepdims=True)
    wb_col = jnp.sum(jnp.where(eye, wb, 0.0), axis=1, keepdims=True)
    lane = lax.broadcasted_iota(jnp.int32, (tm, AUG), 1)
    haug_ref[:, d:d + AUG] = jnp.where(lane == 0, wa_col, jnp.where(lane == 1, wb_col, 0.0))


def _mix_out(o, yf, xs, mod4, layer, wo_bf, wr_hi, wr_lo, rbias, n_lat_tiles, nt, even):
    bsz, t, d = xs.shape
    mod_spec = pl.BlockSpec((None, None, 1, mod4.shape[-1]),
                            lambda b, i: (layer, jnp.where(i >= n_lat_tiles, bsz, b), 0, 0))
    tok = lambda w: pl.BlockSpec((None, TM, w), lambda b, i: (b, i, 0))
    full = lambda a: pl.BlockSpec(a.shape, lambda b, i: (0,) * a.ndim)
    in_specs = [tok(o.shape[-1])] + ([tok(yf.shape[-1])] if even else []) + [
        tok(d), mod_spec, full(wo_bf), full(wr_hi), full(wr_lo), full(rbias)]
    args = [o] + ([yf] if even else []) + [xs, mod4, wo_bf, wr_hi, wr_lo, rbias]
    return pl.pallas_call(
        functools.partial(_mix_out_kernel, even=even, d=d),
        grid=(bsz, nt),
        in_specs=in_specs,
        out_specs=[tok(d), tok(d + AUG),
                   pl.BlockSpec((None, None, 8, TM), lambda b, i: (b, i, 0, 0)),
                   pl.BlockSpec((CLASS_ROWS, LANE), lambda b, i: (0, 0))],
        out_shape=[jax.ShapeDtypeStruct((bsz, nt * TM, d), F32),
                   jax.ShapeDtypeStruct((bsz, nt * TM, d + AUG), F32),
                   jax.ShapeDtypeStruct((bsz, nt, 8, TM), jnp.int32),
                   jax.ShapeDtypeStruct((CLASS_ROWS, LANE), F32)],
        scratch_shapes=[pltpu.VMEM((CLASS_ROWS, LANE), F32)],
        compiler_params=_cparams(("arbitrary", "arbitrary")),
        name="mix_out_router",
    )(*args)


def _row_copy(src, src_row, dst, dst_row, sem):
    return pltpu.make_async_copy(src.at[pl.ds(src_row, 1), :], dst.at[pl.ds(dst_row, 1), :], sem)


def _scatter_kernel(pos_ref, h_ref, xs_in, xs_out, sem):
    del xs_in
    tm = h_ref.shape[0]

    def start(j, carry):
        _row_copy(h_ref, j, xs_out, pos_ref[0, j], sem).start()
        return carry

    def wait(j, carry):
        _row_copy(h_ref, j, xs_out, pos_ref[0, j], sem).wait()
        return carry

    lax.fori_loop(0, tm, start, 0)
    lax.fori_loop(0, tm, wait, 0)


def _dispatch(haug, pos3, nt, cap):
    bsz, t, wa = haug.shape
    tiles_per_batch = t // TM
    zeros = jnp.zeros((cap, wa), F32)
    return pl.pallas_call(
        _scatter_kernel,
        grid=(bsz, nt),
        in_specs=[pl.BlockSpec((None, 1, TM), lambda b, i: (b * nt + i, 0, 0), memory_space=pltpu.SMEM),
                  pl.BlockSpec((TM, wa), lambda b, i: (b * tiles_per_batch + i, 0)),
                  pl.BlockSpec(memory_space=pl.ANY)],
        out_specs=pl.BlockSpec(memory_space=pl.ANY),
        out_shape=jax.ShapeDtypeStruct((cap, wa), F32),
        scratch_shapes=[pltpu.SemaphoreType.DMA(())],
        input_output_aliases={2: 0},
        compiler_params=_cparams(("arbitrary", "arbitrary")),
        name="moe_dispatch",
    )(pos3, haug.reshape(bsz * t, wa), zeros)


def _expert_kernel(ea_ref, eb_ref, nv_ref, x_ref, wga, wua, wda, wgb, wub, wdb, y_ref, *, d):
    del ea_ref, eb_ref
    valid = pl.program_id(0) < nv_ref[0]

    @pl.when(jnp.logical_not(valid))
    def _():
        y_ref[...] = jnp.zeros(y_ref.shape, F32)

    @pl.when(valid)
    def _():
        x = x_ref[:, 0:d].astype(BF16)
        wcol = x_ref[:, d:d + AUG]

        def ffn(wg, wu, wd):
            hg = _dot(x, wg[...])
            hu = _dot(x, wu[...])
            act = (hg * jax.nn.sigmoid(hg)) * hu
            return _dot(act.astype(BF16), wd[...])

        y_ref[...] = wcol[:, 0:1] * ffn(wga, wua, wda) + wcol[:, 1:2] * ffn(wgb, wub, wdb)


def _experts(xsort, tile_ea, tile_eb, nvalid, wg, wu, wd):
    cap, wa = xsort.shape
    d = wa - AUG
    n_tiles = cap // TM
    row = lambda i, ea, eb, nv: (jnp.minimum(i, nv[0] - 1), 0)
    wspec = lambda shape, sel: pl.BlockSpec((None,) + shape, lambda i, ea, eb, nv: ((ea, eb)[sel][i], 0, 0))
    grid_spec = pltpu.PrefetchScalarGridSpec(
        num_scalar_prefetch=3,
        grid=(n_tiles,),
        in_specs=[pl.BlockSpec((TM, wa), row),
                  wspec((d, D_EXPERT), 0), wspec((d, D_EXPERT), 0), wspec((D_EXPERT, d), 0),
                  wspec((d, D_EXPERT), 1), wspec((d, D_EXPERT), 1), wspec((D_EXPERT, d), 1)],
        out_specs=pl.BlockSpec((TM, d), lambda i, ea, eb, nv: (i, 0)),
    )
    return pl.pallas_call(
        functools.partial(_expert_kernel, d=d),
        grid_spec=grid_spec,
        out_shape=jax.ShapeDtypeStruct((cap, d), F32),
        compiler_params=_cparams(("arbitrary",)),
        name="moe_experts",
    )(tile_ea, tile_eb, nvalid, xsort, wg, wu, wd, wg, wu, wd)


def _combine_kernel(pos_ref, y_hbm, x1_ref, mod_ref, o_ref, buf, sem, *, d):
    tm = x1_ref.shape[0]

    def start(j, carry):
        _row_copy(y_hbm, pos_ref[0, j], buf, j, sem).start()
        return carry

    def wait(j, carry):
        _row_copy(y_hbm, pos_ref[0, j], buf, j, sem).wait()
        return carry

    lax.fori_loop(0, tm, start, 0)
    lax.fori_loop(0, tm, wait, 0)
    o_ref[...] = x1_ref[...] + mod_ref[:, 5 * d:6 * d] * buf[...]


def _combine(ysort, pos3, x1, mod4, layer, n_lat_tiles, nt):
    bsz, t, d = x1.shape
    mod_spec = pl.BlockSpec((None, None, 1, mod4.shape[-1]),
                            lambda b, i: (layer, jnp.where(i >= n_lat_tiles, bsz, b), 0, 0))
    return pl.pallas_call(
        functools.partial(_combine_kernel, d=d),
        grid=(bsz, nt),
        in_specs=[pl.BlockSpec((None, 1, TM), lambda b, i: (b * nt + i, 0, 0), memory_space=pltpu.SMEM),
                  pl.BlockSpec(memory_space=pl.ANY),
                  pl.BlockSpec((None, TM, d), lambda b, i: (b, i, 0)),
                  mod_spec],
        out_specs=pl.BlockSpec((None, TM, d), lambda b, i: (b, i, 0)),
        out_shape=jax.ShapeDtypeStruct((bsz, t, d), F32),
        scratch_shapes=[pltpu.VMEM((TM, d), F32), pltpu.SemaphoreType.DMA(())],
        compiler_params=_cparams(("arbitrary", "arbitrary")),
        name="moe_combine",
    )(pos3, ysort, x1, mod4)


def _moe(haug, ri, counts, x1, mod4, layer, wg, wu, wd, n_lat_tiles, nt):
    bsz = haug.shape[0]
    n_tok = bsz * nt * TM
    n_tiles = n_tok // TM + N_CLASSES
    cls = ri[:, :, 0, :].reshape(-1)
    rank = ri[:, :, 1, :].reshape(-1)
    cnt = counts[:N_CLASSES, 0].astype(jnp.int32)
    tiles_per = (cnt + TM - 1) // TM
    tile_end = jnp.cumsum(tiles_per)
    row_off = (tile_end - tiles_per) * TM
    nvalid = tile_end[-1:]
    tile_id = jnp.minimum(jnp.arange(n_tiles, dtype=jnp.int32), nvalid - 1)
    tile_cls = jnp.sum((tile_id[:, None] >= tile_end[None, :]).astype(jnp.int32), axis=1)
    grp, pair = tile_cls // N_PAIRS, tile_cls % N_PAIRS
    tile_ea = grp * EXPERTS_PER_GROUP + jnp.asarray(PAIR_A, jnp.int32)[pair]
    tile_eb = grp * EXPERTS_PER_GROUP + jnp.asarray(PAIR_B, jnp.int32)[pair]
    pos3 = (row_off[cls] + rank).reshape(bsz * nt, 1, TM)
    xsort = _dispatch(haug, pos3, nt, n_tiles * TM)
    ysort = _experts(xsort, tile_ea, tile_eb, nvalid.astype(jnp.int32), wg, wu, wd)
    return _combine(ysort, pos3, x1, mod4, layer, n_lat_tiles, nt)


def _final_kernel(x_ref, g_ref, o_ref):
    x = x_ref[...]
    o_ref[...] = x * lax.rsqrt(jnp.mean(x * x, axis=-1, keepdims=True) + EPS) * g_ref[...]


def _final_norm(xs, gain, n_lat):
    bsz, _, d = xs.shape
    return pl.pallas_call(
        _final_kernel,
        grid=(bsz, n_lat // TM),
        in_specs=[pl.BlockSpec((None, TM, d), lambda b, i: (b, i, 0)),
                  pl.BlockSpec((1, d), lambda b, i: (0, 0))],
        out_specs=pl.BlockSpec((None, TM, d), lambda b, i: (b, i, 0)),
        out_shape=jax.ShapeDtypeStruct((bsz, n_lat, d), F32),
        compiler_params=_cparams(("arbitrary", "arbitrary")),
        name="final_norm",
    )(xs, gain.reshape(1, d))


def _rope_tables(n_lat, n_ctx, dim):
    rows = n_lat // GRID_W
    row = jnp.broadcast_to(jnp.arange(rows, dtype=F32)[:, None], (rows, GRID_W)).reshape(-1)
    col = jnp.broadcast_to(jnp.arange(GRID_W, dtype=F32)[None, :], (rows, GRID_W)).reshape(-1)
    half = dim // 2
    inv_freq = ROPE_THETA ** (-jnp.arange(0, half, 2, dtype=F32) / half)
    ang_r, ang_c = row[:, None] * inv_freq, col[:, None] * inv_freq
    cr, sr, cc, sc = jnp.cos(ang_r), jnp.sin(ang_r), jnp.cos(ang_c), jnp.sin(ang_c)
    cos = jnp.tile(jnp.concatenate([cr, cr, cc, cc], axis=-1), (1, LANE // dim))
    sin = jnp.tile(jnp.concatenate([-sr, sr, -sc, sc], axis=-1), (1, LANE // dim))
    cos = jnp.concatenate([cos, jnp.ones((n_ctx, LANE), F32)], axis=0)
    sin = jnp.concatenate([sin, jnp.zeros((n_ctx, LANE), F32)], axis=0)
    return cos, sin


def _split_bf16(w):
    hi = w.astype(BF16)
    return hi, (w - hi.astype(F32)).astype(BF16)


def kernel(x, c, ctx, c_ctx, w_ada, b_ada, w_in_even, w_out_even, a_q_gain, a_k_gain, b_gain, w_in_odd, w_out_odd, c_lambda_q1, c_lambda_k1, c_lambda_q2, c_lambda_k2, c_head_gain, w_group, b_group, w_router, b_router, w_gate, w_up, w_down, final_gain):
    bsz, n_lat, d = x.shape
    n_ctx = ctx.shape[1]
    depth = w_ada.shape[0]
    assert n_lat % TM == 0 and n_ctx == TM and bsz < 8 and d % LANE == 0
    n_lat_tiles = n_lat // TM
    t = n_lat + n_ctx

    xs = jnp.concatenate([x, ctx], axis=1)
    cond = jnp.zeros((8, d), F32).at[:bsz].set(c).at[bsz].set(c_ctx)
    mod = _modulation(cond, w_ada, b_ada)
    mod4 = mod.reshape(depth, 8, 1, mod.shape[-1])
    rope_a = _rope_tables(n_lat, n_ctx, A_HEAD_DIM)
    rope_c = _rope_tables(n_lat, n_ctx, C_QK_DIM)

    for l in range(depth):
        last = l == depth - 1
        i = l // 2
        even = l % 2 == 0
        nt = n_lat_tiles if last else n_lat_tiles + 1
        if even:
            gains = (a_q_gain[i].reshape(1, -1), a_k_gain[i].reshape(1, -1), b_gain[i].reshape(1, -1))
            q, k, v, gf = _project(xs, mod4, l, w_in_even[i].astype(BF16), *rope_a, gains, n_lat_tiles, True)
            qt = q.transpose(0, 2, 1)
            o = _attention(qt, k, v, even=True, q_row0=0, n_q=n_lat, kv_row0=0, n_kv_rows=t,
                           tq=256, kvt=768, name="attn_even_lat")
            yf = _fourier_lat(gf[:, :n_lat])
            if not last:
                o_c = _attention(qt, k, v, even=True, q_row0=n_lat, n_q=n_ctx, kv_row0=n_lat, n_kv_rows=n_ctx,
                                 tq=n_ctx, kvt=n_ctx, name="attn_even_ctx")
                o = jnp.concatenate([o, o_c], axis=1)
                yf = jnp.concatenate([yf, _fourier_ctx(gf[:, n_lat:])], axis=1)
            w_out = w_out_even[i]
        else:
            lam_init = 0.8 - 0.6 * math.exp(-0.3 * l)
            lam = jnp.stack([c_lambda_q1[i], c_lambda_k1[i], c_lambda_q2[i], c_lambda_k2[i]])
            hg = c_head_gain[i].reshape(1, -1)
            q, k, v = _project(xs, mod4, l, w_in_odd[i].astype(BF16), *rope_c, None, n_lat_tiles, False)
            qt = q.transpose(0, 2, 1)
            o = _attention(qt, k, v, even=False, q_row0=0, n_q=n_lat, kv_row0=0, n_kv_rows=t,
                           tq=512, kvt=768, lam=lam, head_gain=hg, lam_init=lam_init, name="attn_odd_lat")
            yf = None
            if not last:
                o_c = _attention(qt, k, v, even=False, q_row0=n_lat, n_q=n_ctx, kv_row0=n_lat, n_kv_rows=n_ctx,
                                 tq=n_ctx, kvt=n_ctx, lam=lam, head_gain=hg, lam_init=lam_init,
                                 name="attn_odd_ctx")
                o = jnp.concatenate([o, o_c], axis=1)
            w_out = w_out_odd[i]
        w_r = jnp.zeros((CLASS_ROWS, d), F32).at[:N_GROUPS].set(w_group[l].T).at[
            N_GROUPS:N_GROUPS + N_EXPERTS].set(w_router[l].T)
        b_r = jnp.zeros((CLASS_ROWS,), F32).at[:N_GROUPS].set(b_group[l]).at[
            N_GROUPS:N_GROUPS + N_EXPERTS].set(b_router[l])
        wr_hi, wr_lo = _split_bf16(w_r)
        rbias = jnp.broadcast_to(b_r[:, None], (CLASS_ROWS, TM))
        x1, haug, ri, counts = _mix_out(o, yf, xs, mod4, l, w_out.astype(BF16), wr_hi, wr_lo, rbias,
                                        n_lat_tiles, nt, even)
        xs = _moe(haug, ri, counts, x1, mod4, l, w_gate[l].astype(BF16), w_up[l].astype(BF16),
                  w_down[l].astype(BF16), n_lat_tiles, nt)
    return _final_norm(xs, final_gain, n_lat)
```
